```python
import math
import jax, jax.numpy as jnp
from jax import lax
import numpy as np

D_MODEL = 2048
BATCH = 4
SEQ = 2048
DEPTH = 1
DEC_BATCH = 128
DEC_SEQ = 8
PAST_LEN = 16384
PAGE_SIZE = 128

D_MIX = 2 * D_MODEL
CONF_WIDTH = D_MIX // 2
CONF_HEADS = 16
CONF_HEAD_DIM = CONF_WIDTH // CONF_HEADS
CONF_KERNEL = 31
SSM_WIDTH = D_MIX - CONF_WIDTH
SSM_HEAD_DIM = 64
SSM_HEADS = SSM_WIDTH // SSM_HEAD_DIM
SSM_GROUPS = 8
SSM_STATE = 128
SSM_CONV = 4
SSM_CHUNK = 128
SSM_CONV_DIM = SSM_WIDTH + 2 * SSM_GROUPS * SSM_STATE
D_FF = 5632
IN_COLS = 2 * CONF_WIDTH + SSM_WIDTH + SSM_CONV_DIM + SSM_HEADS
ALPHA = (2 * DEPTH) ** 0.25
BETA = (8 * DEPTH) ** -0.25
LN_EPS = 1e-5
RMS_EPS = 1e-5

kernel_name = 'hymba_conformer_ssd_deepnorm_adaln_step'


def layer_norm(x, g, b):
    xf = x.astype(jnp.float32)
    mu = jnp.mean(xf, axis=-1, keepdims=True)
    var = jnp.mean(jnp.square(xf - mu), axis=-1, keepdims=True)
    return ((xf - mu) * lax.rsqrt(var + LN_EPS) * g.astype(jnp.float32) + b.astype(jnp.float32)).astype(x.dtype)


def causal_dwconv(u, prev, w, b):
    ext = jnp.concatenate([prev.astype(u.dtype), u], axis=1)
    y = lax.conv_general_dilated(ext, w[:, None, :].astype(u.dtype), window_strides=(1,), padding='VALID',
                                 dimension_numbers=('NWC', 'WIO', 'NWC'), feature_group_count=u.shape[-1])
    return y + b.astype(u.dtype), ext[:, -(w.shape[0] - 1):]


def segsum(a):
    cs = jnp.cumsum(a, axis=-1)
    diff = cs[..., :, None] - cs[..., None, :]
    t = a.shape[-1]
    mask = jnp.tril(jnp.ones((t, t), dtype=bool))
    return jnp.where(mask, diff, -jnp.inf)


def ssd_chunked(x, dt, a, bmat, cmat, h0):
    f32 = jnp.float32
    bsz, t_len, n_heads, p_dim = x.shape
    L = min(SSM_CHUNK, t_len)
    n_chunks = -(-t_len // L)
    pad = n_chunks * L - t_len
    x, bmat, cmat = x.astype(f32), bmat.astype(f32), cmat.astype(f32)
    if pad:
        padw = lambda v: jnp.pad(v, [(0, 0), (0, pad)] + [(0, 0)] * (v.ndim - 2))
        x, dt, bmat, cmat = padw(x), padw(dt), padw(bmat), padw(cmat)
    G = SSM_GROUPS
    R = n_heads // G
    N = bmat.shape[-1]
    xd = (x * dt[..., None]).reshape(bsz, n_chunks, L, G, R, p_dim)
    bc = bmat.reshape(bsz, n_chunks, L, G, N)
    cc = cmat.reshape(bsz, n_chunks, L, G, N)
    adt = (dt * a).reshape(bsz, n_chunks, L, n_heads).transpose(0, 3, 1, 2)
    a_cs = jnp.cumsum(adt, axis=-1)
    decay_in = jnp.exp(segsum(adt)).reshape(bsz, G, R, n_chunks, L, L)
    cb = jnp.einsum('bclgn,bcsgn->bgcls', cc, bc)
    y_diag = jnp.einsum('bgrcls,bcsgrp->bclgrp', cb[:, :, None] * decay_in, xd)
    decay_to_end = jnp.exp(a_cs[..., -1:] - a_cs).reshape(bsz, G, R, n_chunks, L)
    chunk_states = jnp.einsum('bclgn,bgrcl,bclgrp->bcgrpn', bc, decay_to_end, xd)
    all_states = jnp.concatenate([h0.astype(f32).reshape(bsz, 1, G, R, p_dim, N), chunk_states], axis=1)
    chunk_decay = jnp.exp(segsum(jnp.pad(a_cs[..., -1], [(0, 0), (0, 0), (1, 0)])))
    chunk_decay = chunk_decay.reshape(bsz, G, R, n_chunks + 1, n_chunks + 1)
    states = jnp.einsum('bgrzc,bcgrpn->bzgrpn', chunk_decay, all_states)
    decay_out = jnp.exp(a_cs).reshape(bsz, G, R, n_chunks, L)
    y_off = jnp.einsum('bclgn,bcgrpn,bgrcl->bclgrp', cc, states[:, :-1], decay_out)
    y = (y_diag + y_off).reshape(bsz, n_chunks * L, n_heads, p_dim)[:, :t_len]
    h_final = states[:, -1].reshape(bsz, n_heads, p_dim, N)
    return y, h_final


def swiglu(h, wg, wu, wd):
    return (jax.nn.silu(h @ wg) * (h @ wu)) @ wd


def hybrid_mixer(h, conf_prev, ssm_conv_prev, ssm_h0, p):
    bsz, t_len, _ = h.shape
    proj = h @ p['w_in']
    i1 = CONF_WIDTH
    i2 = 2 * CONF_WIDTH
    i3 = i2 + SSM_WIDTH
    i4 = i3 + SSM_CONV_DIM
    a_val, a_gate, z, xbc, dt_raw = proj[..., :i1], proj[..., i1:i2], proj[..., i2:i3], proj[..., i3:i4], proj[..., i4:]
    u = a_val * jax.nn.sigmoid(a_gate)
    u, conf_buf = causal_dwconv(u, conf_prev, p['conf_conv_w'], p['conf_conv_b'])
    uh = u.reshape(bsz, t_len, CONF_HEADS, CONF_HEAD_DIM)
    uh = layer_norm(uh, p['conf_norm_g'].reshape(CONF_HEADS, CONF_HEAD_DIM), p['conf_norm_b'].reshape(CONF_HEADS, CONF_HEAD_DIM))
    y_conf = jax.nn.silu(uh).reshape(bsz, t_len, CONF_WIDTH)
    xbc, ssm_buf = causal_dwconv(xbc, ssm_conv_prev, p['ssm_conv_w'], p['ssm_conv_b'])
    xbc = jax.nn.silu(xbc)
    xs = xbc[..., :SSM_WIDTH].reshape(bsz, t_len, SSM_HEADS, SSM_HEAD_DIM)
    bs = xbc[..., SSM_WIDTH:SSM_WIDTH + SSM_GROUPS * SSM_STATE].reshape(bsz, t_len, SSM_GROUPS, SSM_STATE)
    cs = xbc[..., SSM_WIDTH + SSM_GROUPS * SSM_STATE:].reshape(bsz, t_len, SSM_GROUPS, SSM_STATE)
    dt = jax.nn.softplus(dt_raw.astype(jnp.float32) + p['ssm_dt_bias'].astype(jnp.float32))
    a = -jnp.exp(p['ssm_a_log'].astype(jnp.float32))
    y, h_final = ssd_chunked(xs, dt, a, bs, cs, ssm_h0)
    y = y + p['ssm_d'].astype(jnp.float32)[:, None] * xs.astype(jnp.float32)
    yg = (y.reshape(bsz, t_len, SSM_WIDTH) * jax.nn.silu(z.astype(jnp.float32)))
    yg = yg.reshape(bsz, t_len, SSM_GROUPS, SSM_WIDTH // SSM_GROUPS)
    yg = yg * lax.rsqrt(jnp.mean(jnp.square(yg), axis=-1, keepdims=True) + RMS_EPS)
    y_ssm = (yg.reshape(bsz, t_len, SSM_WIDTH) * p['ssm_norm_g'].astype(jnp.float32)).astype(h.dtype)
    out = jnp.concatenate([y_conf, y_ssm], axis=-1) @ p['w_out']
    return out, conf_buf, ssm_buf, h_final.astype(h.dtype)


def trunk_layer(x, c, conf_prev, ssm_conv_prev, ssm_h0, p):
    mod = (jax.nn.silu(c) @ p['w_ada'] + p['b_ada'])[:, None, :]
    s1, k1, g1, s2, k2, g2, s3, k3, g3 = jnp.split(mod, 9, axis=-1)
    h = x * (1 + k1) + s1
    x = layer_norm(ALPHA * x + g1 * (0.5 * swiglu(h, p['ffn1_w_gate'], p['ffn1_w_up'], p['ffn1_w_down'])), p['ln1_g'], p['ln1_b'])
    h = x * (1 + k2) + s2
    m, conf_buf, ssm_buf, h_final = hybrid_mixer(h, conf_prev, ssm_conv_prev, ssm_h0, p)
    x = layer_norm(ALPHA * x + g2 * m, p['ln2_g'], p['ln2_b'])
    h = x * (1 + k3) + s3
    x = layer_norm(ALPHA * x + g3 * (0.5 * swiglu(h, p['ffn2_w_gate'], p['ffn2_w_up'], p['ffn2_w_down'])), p['ln3_g'], p['ln3_b'])
    return x, conf_buf, ssm_buf, h_final


def setup_inputs(seed: int = 0) -> dict:
    key = jax.random.key(seed)
    ks = iter(jax.random.split(key, 48))
    f32 = jnp.float32

    def nrm(shape, scale):
        return jax.random.normal(next(ks), shape, f32) * scale

    def gain(shape):
        return 1.0 + nrm(shape, 0.02)

    dt_u = jax.random.uniform(next(ks), (DEPTH, SSM_HEADS), f32)
    dt0 = jnp.exp(dt_u * (math.log(0.1) - math.log(0.001)) + math.log(0.001))
    a_init = jax.random.uniform(next(ks), (DEPTH, SSM_HEADS), f32, 1.0, 16.0)
    return {
        'x_prompt': nrm((BATCH, SEQ, D_MODEL), 1.0),
        'x_sample': nrm((DEC_BATCH, DEC_SEQ, D_MODEL), 1.0),
        'c_prompt': nrm((BATCH, D_MODEL), 1.0),
        'c_sample': nrm((DEC_BATCH, D_MODEL), 1.0),
        'state_conf_conv': nrm((DEPTH, DEC_BATCH, CONF_KERNEL - 1, CONF_WIDTH), 0.5),
        'state_ssm_conv': nrm((DEPTH, DEC_BATCH, SSM_CONV - 1, SSM_CONV_DIM), 1.0),
        'state_ssm': nrm((DEPTH, DEC_BATCH, SSM_HEADS, SSM_HEAD_DIM, SSM_STATE), 0.1),
        'ln_in_g': gain((D_MODEL,)),
        'ln_in_b': nrm((D_MODEL,), 0.02),
        'w_ada': nrm((DEPTH, D_MODEL, 9 * D_MODEL), 0.5 * D_MODEL ** -0.5),
        'b_ada': nrm((DEPTH, 9 * D_MODEL), 0.01),
        'ffn1_w_gate': nrm((DEPTH, D_MODEL, D_FF), D_MODEL ** -0.5),
        'ffn1_w_up': nrm((DEPTH, D_MODEL, D_FF), D_MODEL ** -0.5),
        'ffn1_w_down': nrm((DEPTH, D_FF, D_MODEL), BETA * D_FF ** -0.5),
        'ln1_g': gain((DEPTH, D_MODEL)),
        'ln1_b': nrm((DEPTH, D_MODEL), 0.02),
        'w_in': nrm((DEPTH, D_MODEL, IN_COLS), D_MODEL ** -0.5),
        'conf_conv_w': nrm((DEPTH, CONF_KERNEL, CONF_WIDTH), CONF_KERNEL ** -0.5),
        'conf_conv_b': nrm((DEPTH, CONF_WIDTH), 0.02),
        'conf_norm_g': gain((DEPTH, CONF_WIDTH)),
        'conf_norm_b': nrm((DEPTH, CONF_WIDTH), 0.02),
        'ssm_conv_w': nrm((DEPTH, SSM_CONV, SSM_CONV_DIM), SSM_CONV ** -0.5),
        'ssm_conv_b': nrm((DEPTH, SSM_CONV_DIM), 0.02),
        'ssm_dt_bias': dt0 + jnp.log(-jnp.expm1(-dt0)),
        'ssm_a_log': jnp.log(a_init),
        'ssm_d': gain((DEPTH, SSM_HEADS)),
        'ssm_norm_g': gain((DEPTH, SSM_WIDTH)),
        'w_out': nrm((DEPTH, D_MIX, D_MODEL), BETA * D_MIX ** -0.5),
        'ln2_g': gain((DEPTH, D_MODEL)),
        'ln2_b': nrm((DEPTH, D_MODEL), 0.02),
        'ffn2_w_gate': nrm((DEPTH, D_MODEL, D_FF), D_MODEL ** -0.5),
        'ffn2_w_up': nrm((DEPTH, D_MODEL, D_FF), D_MODEL ** -0.5),
        'ffn2_w_down': nrm((DEPTH, D_FF, D_MODEL), BETA * D_FF ** -0.5),
        'ln3_g': gain((DEPTH, D_MODEL)),
        'ln3_b': nrm((DEPTH, D_MODEL), 0.02),
    }


def reference(x_prompt, x_sample, c_prompt, c_sample, state_conf_conv, state_ssm_conv, state_ssm,
              ln_in_g, ln_in_b, w_ada, b_ada, ffn1_w_gate, ffn1_w_up, ffn1_w_down, ln1_g, ln1_b,
              w_in, conf_conv_w, conf_conv_b, conf_norm_g, conf_norm_b, ssm_conv_w, ssm_conv_b,
              ssm_dt_bias, ssm_a_log, ssm_d, ssm_norm_g, w_out, ln2_g, ln2_b,
              ffn2_w_gate, ffn2_w_up, ffn2_w_down, ln3_g, ln3_b):
    xp = layer_norm(x_prompt, ln_in_g, ln_in_b)
    xs = layer_norm(x_sample, ln_in_g, ln_in_b)
    bp = x_prompt.shape[0]
    zero_conf = jnp.zeros((bp, CONF_KERNEL - 1, CONF_WIDTH), x_prompt.dtype)
    zero_sconv = jnp.zeros((bp, SSM_CONV - 1, SSM_CONV_DIM), x_prompt.dtype)
    zero_h = jnp.zeros((bp, SSM_HEADS, SSM_HEAD_DIM, SSM_STATE), x_prompt.dtype)
    pc, ps, ph, sc, ss, sh = [], [], [], [], [], []
    for l in range(DEPTH):
        p = dict(w_ada=w_ada[l], b_ada=b_ada[l], ffn1_w_gate=ffn1_w_gate[l], ffn1_w_up=ffn1_w_up[l],
                 ffn1_w_down=ffn1_w_down[l], ln1_g=ln1_g[l], ln1_b=ln1_b[l], w_in=w_in[l],
                 conf_conv_w=conf_conv_w[l], conf_conv_b=conf_conv_b[l], conf_norm_g=conf_norm_g[l],
                 conf_norm_b=conf_norm_b[l], ssm_conv_w=ssm_conv_w[l], ssm_conv_b=ssm_conv_b[l],
                 ssm_dt_bias=ssm_dt_bias[l], ssm_a_log=ssm_a_log[l], ssm_d=ssm_d[l], ssm_norm_g=ssm_norm_g[l],
                 w_out=w_out[l], ln2_g=ln2_g[l], ln2_b=ln2_b[l], ffn2_w_gate=ffn2_w_gate[l],
                 ffn2_w_up=ffn2_w_up[l], ffn2_w_down=ffn2_w_down[l], ln3_g=ln3_g[l], ln3_b=ln3_b[l])
        xp, a1, a2, a3 = trunk_layer(xp, c_prompt, zero_conf, zero_sconv, zero_h, p)
        xs, b1, b2, b3 = trunk_layer(xs, c_sample, state_conf_conv[l], state_ssm_conv[l], state_ssm[l], p)
        pc.append(a1); ps.append(a2); ph.append(a3)
        sc.append(b1); ss.append(b2); sh.append(b3)
    return (xp, xs, jnp.stack(pc), jnp.stack(ps), jnp.stack(ph), jnp.stack(sc), jnp.stack(ss), jnp.stack(sh))
```

```python
import functools

import jax
import jax.numpy as jnp
from jax import lax
from jax.experimental import pallas as pl
from jax.experimental.pallas import tpu as pltpu

F32 = jnp.float32
BF16 = jnp.bfloat16

LN_EPS = 1e-5
RMS_EPS = 1e-5
CONF_HEADS = 16
SSM_HEAD_DIM = 64
SSM_GROUPS = 8
SSM_CHUNK = 128

LANES = 128
SUBLANES = 8
VMEM_LIMIT_BYTES = 56 * 2**20

FFN_TM = 512
FFN_TF = 512
INPROJ_STEPS = 4
CONV_ROWS = 16
SAMPLE_SEQS = 4
CONV1_HALO = 32
CONV2_HALO = 8


def _dot(a, b):
    return jnp.dot(a, b, preferred_element_type=F32)


def _dot_nt(a, b):
    return lax.dot_general(a, b, (((1,), (1,)), ((), ())), preferred_element_type=F32)


def _dot_tn(a, b):
    return lax.dot_general(a, b, (((0,), (0,)), ((), ())), preferred_element_type=F32)


def _sigmoid(v):
    return 1.0 / (1.0 + jnp.exp(-v))


def _silu(v):
    return v * _sigmoid(v)


def _softplus(v):
    return jnp.maximum(v, 0.0) + jnp.log(1.0 + jnp.exp(-jnp.abs(v)))


def _layer_norm(x, g, b):
    mu = jnp.mean(x, axis=-1, keepdims=True)
    xc = x - mu
    var = jnp.mean(xc * xc, axis=-1, keepdims=True)
    return xc * lax.rsqrt(var + LN_EPS) * g + b


def _split3(v):
    hi = v.astype(BF16)
    r1 = v - hi.astype(F32)
    mid = r1.astype(BF16)
    lo = (r1 - mid.astype(F32)).astype(BF16)
    return hi, mid, lo


def _sel_dot(sel, v):
    w = v.shape[1]
    r = _dot(sel, jnp.concatenate(_split3(v), axis=1))
    return r[:, :w] + r[:, w:2 * w] + r[:, 2 * w:]


def _dot_sel(v, sel):
    p = v.shape[0]
    r = _dot(jnp.concatenate(_split3(v), axis=0), sel)
    return r[:p] + r[p:2 * p] + r[2 * p:]


def _onehot(cond):
    return jnp.where(cond, 1.0, 0.0).astype(BF16)


def _seq_rows(ref, tm):
    m = ref[...]
    if m.ndim == 3:
        m = m[0]
    nseq = m.shape[0]
    if nseq == 1:
        return m
    rps = tm // nseq
    r = lax.broadcasted_iota(jnp.int32, (tm, nseq), 0)
    c = lax.broadcasted_iota(jnp.int32, (tm, nseq), 1)
    return _sel_dot(_onehot((r >= c * rps) & (r < (c + 1) * rps)), m)


def _head_expander(n_rows, width, head_dim):
    h = lax.broadcasted_iota(jnp.int32, (n_rows, width), 0)
    c = lax.broadcasted_iota(jnp.int32, (n_rows, width), 1)
    return _onehot((c >= h * head_dim) & (c < (h + 1) * head_dim))


def _ada_kernel(c_ref, w_ref, b_ref, o_ref):
    c = c_ref[...]
    o_ref[...] = _dot(_silu(c).astype(BF16), w_ref[...].astype(BF16)) + b_ref[...]


def _ada_call(c_all, w, b):
    rows, d = c_all.shape
    n = w.shape[1]
    tn = 1024
    return pl.pallas_call(
        _ada_kernel,
        grid=(n // tn,),
        in_specs=[pl.BlockSpec((rows, d), lambda j: (0, 0)),
                  pl.BlockSpec((d, tn), lambda j: (0, j)),
                  pl.BlockSpec((1, tn), lambda j: (0, j))],
        out_specs=pl.BlockSpec((rows, tn), lambda j: (0, j)),
        out_shape=jax.ShapeDtypeStruct((rows, n), F32),
        compiler_params=pltpu.CompilerParams(dimension_semantics=("arbitrary",),
                                             vmem_limit_bytes=VMEM_LIMIT_BYTES),
        name="ada_mod",
    )(c_all, w, b.reshape(1, n))


class _Group:
    def __init__(self, mod, n_seq, seq_len, tm):
        self.n_seq, self.seq_len, self.tm = n_seq, seq_len, tm
        d9 = mod.shape[-1]
        if seq_len >= tm:
            assert seq_len % tm == 0
            self.mod = mod.reshape(n_seq, 1, d9)
            self.tile_seqs = 1
        else:
            assert tm % seq_len == 0 and (n_seq * seq_len) % tm == 0
            self.mod = mod
            self.tile_seqs = tm // seq_len

    def mod_spec(self, d, chunk):
        if self.tile_seqs == 1:
            tps = self.seq_len // self.tm
            return pl.BlockSpec((1, 1, d), lambda i, j: (i // tps, 0, chunk))
        return pl.BlockSpec((self.tile_seqs, d), lambda i, j: (i, chunk))


def _ffn_kernel(*refs, pre_ln, alpha):
    if pre_ln:
        (x_ref, k_ref, s_ref, g_ref, pg_ref, pb_ref, wg_ref, wu_ref, wd_ref, lg_ref, lb_ref,
         o_ref, h_scr, acc_scr, xn_scr) = refs
    else:
        (x_ref, k_ref, s_ref, g_ref, wg_ref, wu_ref, wd_ref, lg_ref, lb_ref,
         o_ref, h_scr, acc_scr) = refs
    j = pl.program_id(1)
    tm = x_ref.shape[0]

    @pl.when(j == 0)
    def _():
        x = x_ref[...]
        if pre_ln:
            x = _layer_norm(x, pg_ref[...], pb_ref[...])
            xn_scr[...] = x
        h = x * (1.0 + _seq_rows(k_ref, tm)) + _seq_rows(s_ref, tm)
        h_scr[...] = h.astype(BF16)
        acc_scr[...] = jnp.zeros_like(acc_scr)

    h = h_scr[...]
    gt = _dot(h, wg_ref[...])
    ut = _dot(h, wu_ref[...])
    a = (_silu(gt) * ut).astype(BF16)
    acc_scr[...] += _dot(a, wd_ref[...])

    @pl.when(j == pl.num_programs(1) - 1)
    def _():
        x = xn_scr[...] if pre_ln else x_ref[...]
        y = alpha * x + _seq_rows(g_ref, tm) * (0.5 * acc_scr[...])
        o_ref[...] = _layer_norm(y, lg_ref[...], lb_ref[...])


def _ffn_call(x, grp, chunks, wg, wu, wd, ln_g, ln_b, pre_ln, alpha, name):
    m, d = x.shape
    f = wg.shape[1]
    tm, tf = grp.tm, FFN_TF
    row = pl.BlockSpec((1, d), lambda i, j: (0, 0))
    in_specs = [pl.BlockSpec((tm, d), lambda i, j: (i, 0)),
                grp.mod_spec(d, chunks[0] + 1), grp.mod_spec(d, chunks[0]), grp.mod_spec(d, chunks[0] + 2)]
    args = [x, grp.mod, grp.mod, grp.mod]
    if pre_ln is not None:
        in_specs += [row, row]
        args += [pre_ln[0].reshape(1, d), pre_ln[1].reshape(1, d)]
    in_specs += [pl.BlockSpec((d, tf), lambda i, j: (0, j)),
                 pl.BlockSpec((d, tf), lambda i, j: (0, j)),
                 pl.BlockSpec((tf, d), lambda i, j: (j, 0)),
                 row, row]
    args += [wg, wu, wd, ln_g.reshape(1, d), ln_b.reshape(1, d)]
    scratch = [pltpu.VMEM((tm, d), BF16), pltpu.VMEM((tm, d), F32)]
    if pre_ln is not None:
        scratch.append(pltpu.VMEM((tm, d), F32))
    return pl.pallas_call(
        functools.partial(_ffn_kernel, pre_ln=pre_ln is not None, alpha=alpha),
        grid=(m // tm, f // tf),
        in_specs=in_specs,
        out_specs=pl.BlockSpec((tm, d), lambda i, j: (i, 0)),
        out_shape=jax.ShapeDtypeStruct((m, d), F32),
        scratch_shapes=scratch,
        compiler_params=pltpu.CompilerParams(dimension_semantics=("arbitrary", "arbitrary"),
                                             vmem_limit_bytes=VMEM_LIMIT_BYTES),
        name=name,
    )(*args)


def _inproj_kernel(x_ref, k_ref, s_ref, wv_ref, wgt_ref, wz_ref, wx_ref, wdt_ref, dtb_ref,
                   u_ref, zs_ref, xbc_ref, dt_ref, h_scr, *, n_heads):
    j = pl.program_id(1)
    tm = x_ref.shape[0]

    @pl.when(j == 0)
    def _():
        h = x_ref[...] * (1.0 + _seq_rows(k_ref, tm)) + _seq_rows(s_ref, tm)
        hb = h.astype(BF16)
        h_scr[...] = hb
        dt = _softplus(_dot(hb, wdt_ref[...]) + dtb_ref[...])
        lane = lax.broadcasted_iota(jnp.int32, dt.shape, 1)
        dt_ref[...] = jnp.where(lane < n_heads, dt, 0.0)

    hb = h_scr[...]
    u_ref[...] = _dot(hb, wv_ref[...]) * _sigmoid(_dot(hb, wgt_ref[...]))
    zs_ref[...] = _silu(_dot(hb, wz_ref[...]))
    xbc_ref[...] = _dot(hb, wx_ref[...])


def _inproj_call(x, grp, w_in, w_dt, dt_bias, conf_w, ssm_w, xbc_w, name):
    m, d = x.shape
    tm, nj = grp.tm, INPROJ_STEPS
    tc, tz, tx = conf_w // nj, ssm_w // nj, xbc_w // nj
    assert tc % LANES == 0 and tz % LANES == 0 and tx % LANES == 0
    assert conf_w % tc == 0 and (2 * conf_w) % tz == 0 and (2 * conf_w + ssm_w) % tx == 0
    gate0, z0, x0 = conf_w // tc, 2 * conf_w // tz, (2 * conf_w + ssm_w) // tx
    n_heads = dt_bias.shape[0]
    dtb = jnp.zeros((1, LANES), F32).at[0, :n_heads].set(dt_bias)
    const = lambda i, j: (0, 0)
    return pl.pallas_call(
        functools.partial(_inproj_kernel, n_heads=n_heads),
        grid=(m // tm, nj),
        in_specs=[pl.BlockSpec((tm, d), lambda i, j: (i, 0)),
                  grp.mod_spec(d, 4), grp.mod_spec(d, 3),
                  pl.BlockSpec((d, tc), lambda i, j: (0, j)),
                  pl.BlockSpec((d, tc), lambda i, j: (0, gate0 + j)),
                  pl.BlockSpec((d, tz), lambda i, j: (0, z0 + j)),
                  pl.BlockSpec((d, tx), lambda i, j: (0, x0 + j)),
                  pl.BlockSpec((d, LANES), const),
                  pl.BlockSpec((1, LANES), const)],
        out_specs=[pl.BlockSpec((tm, tc), lambda i, j: (i, j)),
                   pl.BlockSpec((tm, tz), lambda i, j: (i, j)),
                   pl.BlockSpec((tm, tx), lambda i, j: (i, j)),
                   pl.BlockSpec((tm, LANES), lambda i, j: (i, 0))],
        out_shape=[jax.ShapeDtypeStruct((m, conf_w), F32),
                   jax.ShapeDtypeStruct((m, ssm_w), F32),
                   jax.ShapeDtypeStruct((m, xbc_w), F32),
                   jax.ShapeDtypeStruct((m, LANES), F32)],
        scratch_shapes=[pltpu.VMEM((tm, d), BF16)],
        compiler_params=pltpu.CompilerParams(dimension_semantics=("arbitrary", "arbitrary"),
                                             vmem_limit_bytes=VMEM_LIMIT_BYTES),
        name=name,
    )(x, grp.mod, grp.mod, w_in, w_in, w_in, w_in, w_dt, dtb)


def _conf_norm_act(acc, gam, bet):
    return _silu(_layer_norm(acc, gam, bet))


def _ssd_prep(dt, alog_ref, n_heads, cum_sel):
    lane = lax.broadcasted_iota(jnp.int32, (1, LANES), 1)
    a = jnp.where(lane < n_heads, -jnp.exp(alog_ref[...]), 0.0)
    return _sel_dot(cum_sel, dt * a)


def _ssd_intra_group(g, a_cs, a_cs_t, allowed, bg, cg, xd, heads_per_group, head_dim):
    cb = _dot_nt(cg, bg)
    lane = lax.broadcasted_iota(jnp.int32, xd.shape, 1)
    out = None
    for r in range(heads_per_group):
        h = g * heads_per_group + r
        seg = a_cs[:, h:h + 1] - a_cs_t[h:h + 1, :]
        decay = jnp.exp(jnp.where(allowed, seg, -jnp.inf))
        m = (cb * decay).astype(BF16)
        in_head = (lane >= r * head_dim) & (lane < (r + 1) * head_dim)
        part = _dot(m, jnp.where(in_head, xd, 0.0).astype(BF16))
        out = part if out is None else out + part
    return out


def _gate_and_norm(y_scr, xs, dskip_ref, zs_ref, rmsg_ref, ycat_ref, col0, n_groups, expander):
    width = xs.shape[1]
    gw = width // n_groups
    dsk = _dot_sel(jnp.broadcast_to(dskip_ref[...], (SUBLANES, LANES)), expander)[0:1]
    yz = (y_scr[...] + dsk * xs) * zs_ref[...]
    for g in range(n_groups):
        blk = yz[:, g * gw:(g + 1) * gw]
        ms = jnp.mean(blk * blk, axis=-1, keepdims=True)
        ycat_ref[:, col0 + g * gw:col0 + (g + 1) * gw] = (
            blk * lax.rsqrt(ms + RMS_EPS) * rmsg_ref[:, g * gw:(g + 1) * gw]).astype(BF16)


def _mix_prompt_kernel(u_ref, zs_ref, xbc_ref, dt_ref, cw_ref, cb_ref, cng_ref, cnb_ref, sw_ref, sb_ref,
                       alog_ref, dskip_ref, rmsg_ref,
                       ycat_ref, cst_ref, sst_ref, hst_ref,
                       ubuf, xbuf, act_scr, y_scr, st_scr, *, n_heads):
    c = pl.program_id(1)
    last = pl.num_programs(1) - 1
    L, conf_w = u_ref.shape
    ssm_w = zs_ref.shape[1]
    xbc_w = xbc_ref.shape[1]
    k1, k2 = cw_ref.shape[0], sw_ref.shape[0]
    head_dim = ssm_w // n_heads
    hpg = n_heads // SSM_GROUPS
    gw = hpg * head_dim
    n_state = (xbc_w - ssm_w) // (2 * SSM_GROUPS)

    @pl.when(c == 0)
    def _():
        ubuf[0:CONV1_HALO, :] = jnp.zeros((CONV1_HALO, conf_w), F32)
        xbuf[0:CONV2_HALO, :] = jnp.zeros((CONV2_HALO, xbc_w), F32)
        st_scr[...] = jnp.zeros_like(st_scr)

    ubuf[CONV1_HALO:CONV1_HALO + L, :] = u_ref[...]
    xbuf[CONV2_HALO:CONV2_HALO + L, :] = xbc_ref[...]

    off1 = CONV1_HALO - (k1 - 1)
    for hd in range(conf_w // LANES):
        ls = slice(hd * LANES, (hd + 1) * LANES)
        taps = [cw_ref[k:k + 1, ls] for k in range(k1)]
        bias, gam, bet = cb_ref[:, ls], cng_ref[:, ls], cnb_ref[:, ls]

        for r0 in range(0, L, CONV_ROWS):
            acc = jnp.broadcast_to(bias, (CONV_ROWS, LANES))
            for k in range(k1):
                acc = acc + taps[k] * ubuf[r0 + off1 + k:r0 + off1 + k + CONV_ROWS, ls]
            ycat_ref[r0:r0 + CONV_ROWS, ls] = _conf_norm_act(acc, gam, bet).astype(BF16)

    off2 = CONV2_HALO - (k2 - 1)
    cwid = 4 * LANES
    for cj in range(xbc_w // cwid):
        ls = slice(cj * cwid, (cj + 1) * cwid)
        taps = [sw_ref[k:k + 1, ls] for k in range(k2)]
        bias = sb_ref[:, ls]

        for r0 in range(0, L, CONV_ROWS):
            acc = jnp.broadcast_to(bias, (CONV_ROWS, cwid))
            for k in range(k2):
                acc = acc + taps[k] * xbuf[r0 + off2 + k:r0 + off2 + k + CONV_ROWS, ls]
            act_scr[r0:r0 + CONV_ROWS, ls] = _silu(acc)

    @pl.when(c == last)
    def _():
        cst_ref[0] = u_ref[L - (k1 - 1):L, :]
        sst_ref[0] = xbc_ref[L - (k2 - 1):L, :]

    ubuf[0:CONV1_HALO, :] = ubuf[L:L + CONV1_HALO, :]
    xbuf[0:CONV2_HALO, :] = xbuf[L:L + CONV2_HALO, :]

    rr = lax.broadcasted_iota(jnp.int32, (L, L), 0)
    cc = lax.broadcasted_iota(jnp.int32, (L, L), 1)
    causal = rr >= cc
    dt = dt_ref[...]
    a_cs = _ssd_prep(dt, alog_ref, n_heads, _onehot(causal))
    a_cs_t = a_cs.T
    expander = _head_expander(LANES, ssm_w, head_dim)
    wide = _dot(jnp.concatenate([dt, jnp.exp(a_cs), jnp.exp(a_cs[L - 1:L, :] - a_cs)], axis=0).astype(BF16),
                expander)
    dt_e, dout_e, dte_e = wide[0:L], wide[L:2 * L], wide[2 * L:3 * L]
    xs = act_scr[:, 0:ssm_w]
    xd = xs * dt_e
    xdd = (xd * dte_e).astype(BF16)
    for g in range(SSM_GROUPS):
        gs = slice(g * gw, (g + 1) * gw)
        bg = act_scr[:, ssm_w + g * n_state:ssm_w + (g + 1) * n_state]
        cg = act_scr[:, ssm_w + (SSM_GROUPS + g) * n_state:ssm_w + (SSM_GROUPS + g + 1) * n_state]
        bgb, cgb = bg.astype(BF16), cg.astype(BF16)
        st = st_scr[g]
        y_off = _dot(cgb, st.astype(BF16)) * dout_e[:, gs]
        y_scr[:, gs] = y_off + _ssd_intra_group(g, a_cs, a_cs_t, causal, bgb, cgb, xd[:, gs], hpg, head_dim)
        st_scr[g] = st * dout_e[L - 1:L, gs] + _dot_tn(bgb, xdd[:, gs])

    _gate_and_norm(y_scr, xs, dskip_ref, zs_ref, rmsg_ref, ycat_ref, conf_w, SSM_GROUPS, expander)

    @pl.when(c == last)
    def _():
        for g in range(SSM_GROUPS):
            st = st_scr[g]
            for q in range(gw // LANES):
                hst_ref[0, g * gw + q * LANES:g * gw + (q + 1) * LANES, :] = st[:, q * LANES:(q + 1) * LANES].T


def _mix_prompt_call(u, zs, xbc, dt, n_seq, seq_len, p, name):
    m, conf_w = u.shape
    ssm_w, xbc_w = zs.shape[1], xbc.shape[1]
    n_heads = p["n_heads"]
    L = SSM_CHUNK
    assert seq_len % L == 0 and conf_w // CONF_HEADS == LANES
    nc = seq_len // L
    k1, k2 = p["conf_conv_w"].shape[0], p["ssm_conv_w"].shape[0]
    assert k1 - 1 <= CONV1_HALO <= L and k2 - 1 <= CONV2_HALO <= L
    n_state = (xbc_w - ssm_w) // (2 * SSM_GROUPS)
    gw = ssm_w // SSM_GROUPS
    rows = lambda b, c: (b * nc + c, 0)
    const = lambda b, c: (0, 0)
    per_seq = lambda b, c: (b, 0, 0)
    full = lambda a: pl.BlockSpec(a.shape, const)
    params = [p["conf_conv_w"], p["conf_conv_b"], p["conf_norm_g"], p["conf_norm_b"], p["ssm_conv_w"],
              p["ssm_conv_b"], p["a_log"], p["d_skip"], p["ssm_norm_g"]]
    return pl.pallas_call(
        functools.partial(_mix_prompt_kernel, n_heads=n_heads),
        grid=(n_seq, nc),
        in_specs=[pl.BlockSpec((L, conf_w), rows), pl.BlockSpec((L, ssm_w), rows),
                  pl.BlockSpec((L, xbc_w), rows), pl.BlockSpec((L, LANES), rows)] + [full(a) for a in params],
        out_specs=[pl.BlockSpec((L, conf_w + ssm_w), rows),
                   pl.BlockSpec((1, k1 - 1, conf_w), per_seq),
                   pl.BlockSpec((1, k2 - 1, xbc_w), per_seq),
                   pl.BlockSpec((1, ssm_w, n_state), per_seq)],
        out_shape=[jax.ShapeDtypeStruct((m, conf_w + ssm_w), BF16),
                   jax.ShapeDtypeStruct((n_seq, k1 - 1, conf_w), F32),
                   jax.ShapeDtypeStruct((n_seq, k2 - 1, xbc_w), F32),
                   jax.ShapeDtypeStruct((n_seq, ssm_w, n_state), F32)],
        scratch_shapes=[pltpu.VMEM((L + CONV1_HALO, conf_w), F32),
                        pltpu.VMEM((L + CONV2_HALO, xbc_w), F32),
                        pltpu.VMEM((L, xbc_w), F32),
                        pltpu.VMEM((L, ssm_w), F32),
                        pltpu.VMEM((SSM_GROUPS, n_state, gw), F32)],
        compiler_params=pltpu.CompilerParams(dimension_semantics=("arbitrary", "arbitrary"),
                                             vmem_limit_bytes=VMEM_LIMIT_BYTES),
        name=name,
    )(u, zs, xbc, dt, *params)


def _mix_sample_kernel(u_ref, zs_ref, xbc_ref, dt_ref, cst_ref, sst_ref, hst_ref,
                       cw_ref, cb_ref, cng_ref, cnb_ref, sw_ref, sb_ref, alog_ref, dskip_ref, rmsg_ref,
                       ycat_ref, ncst_ref, nsst_ref, nhst_ref,
                       ext1, ext2, act_scr, y_scr, *, n_heads, seq_len):
    rows, conf_w = u_ref.shape
    ssm_w = zs_ref.shape[1]
    xbc_w = xbc_ref.shape[1]
    k1, k2 = cw_ref.shape[0], sw_ref.shape[0]
    t = seq_len
    nseq = rows // t
    head_dim = ssm_w // n_heads
    hpg = n_heads // SSM_GROUPS
    gw = hpg * head_dim
    n_state = (xbc_w - ssm_w) // (2 * SSM_GROUPS)
    lj = ext_rows = LANES
    pair = 2 * t

    for s in range(nseq):
        ext1[s, 0:k1 - 1, :] = cst_ref[s]
        ext1[s, k1 - 1:k1 - 1 + t, :] = u_ref[s * t:(s + 1) * t, :]
        ncst_ref[s] = ext1[s, t:t + k1 - 1, :]
        ext2[s, 0:k2 - 1, :] = sst_ref[s]
        ext2[s, k2 - 1:k2 - 1 + t, :] = xbc_ref[s * t:(s + 1) * t, :]
        nsst_ref[s] = ext2[s, t:t + k2 - 1, :]

    for hd in range(conf_w // LANES):
        ls = slice(hd * LANES, (hd + 1) * LANES)
        taps = [cw_ref[k:k + 1, ls] for k in range(k1)]
        bias, gam, bet = cb_ref[:, ls], cng_ref[:, ls], cnb_ref[:, ls]
        for sp in range(nseq // 2):
            halves = []
            for s in (2 * sp, 2 * sp + 1):
                acc = jnp.broadcast_to(bias, (t, LANES))
                for k in range(k1):
                    acc = acc + taps[k] * ext1[s, k:k + t, ls]
                halves.append(acc)
            acc = jnp.concatenate(halves, axis=0)
            ycat_ref[sp * pair:(sp + 1) * pair, ls] = _conf_norm_act(acc, gam, bet).astype(BF16)

    for s in range(nseq):
        acc = jnp.broadcast_to(sb_ref[...], (t, xbc_w))
        for k in range(k2):
            acc = acc + sw_ref[k:k + 1, :] * ext2[s, k:k + t, :]
        act_scr[s * t:(s + 1) * t, :] = _silu(acc)

    shift = t.bit_length() - 1
    ri = lax.broadcasted_iota(jnp.int32, (rows, lj), 0)
    cj = lax.broadcasted_iota(jnp.int32, (rows, lj), 1)
    same_seq = lax.shift_right_logical(ri, shift) == lax.shift_right_logical(cj, shift)
    allowed = same_seq & (ri >= cj)
    zpad = jnp.zeros((lj - rows, LANES), F32)
    dt = dt_ref[...]
    dt_p = jnp.concatenate([dt, zpad], axis=0)
    a_cs = _ssd_prep(dt_p, alog_ref, n_heads, _onehot(allowed))
    a_tot = _ssd_prep(dt_p, alog_ref, n_heads, _onehot(same_seq))
    a_cs_t = jnp.concatenate([a_cs, zpad], axis=0).T
    a_tot_t = jnp.concatenate([a_tot, zpad], axis=0).T
    expander = _head_expander(LANES, ssm_w, head_dim)
    wide = _dot(jnp.concatenate([dt, jnp.exp(a_cs), jnp.exp(a_tot - a_cs)], axis=0).astype(BF16), expander)
    dt_e, dout_e, dte_e = wide[0:rows], wide[rows:2 * rows], wide[2 * rows:3 * rows]
    hh = lax.broadcasted_iota(jnp.int32, (ssm_w, LANES), 1)
    rp = lax.broadcasted_iota(jnp.int32, (ssm_w, LANES), 0)
    rows_of_head = _onehot((rp >= hh * head_dim) & (rp < (hh + 1) * head_dim))
    dlast = _sel_dot(rows_of_head, jnp.exp(a_tot_t))

    xs = act_scr[:, 0:ssm_w]
    xd = xs * dt_e
    xdd = xd * dte_e
    zrows = jnp.zeros((lj - rows, gw), F32)
    zstate = jnp.zeros((lj - rows, n_state), F32)
    for g in range(SSM_GROUPS):
        gs = slice(g * gw, (g + 1) * gw)
        bg = act_scr[:, ssm_w + g * n_state:ssm_w + (g + 1) * n_state]
        cg = act_scr[:, ssm_w + (SSM_GROUPS + g) * n_state:ssm_w + (SSM_GROUPS + g + 1) * n_state]
        bg_p = jnp.concatenate([bg, zstate], axis=0).astype(BF16)
        xd_p = jnp.concatenate([xd[:, gs], zrows], axis=0)
        y_scr[:, gs] = _ssd_intra_group(g, a_cs, a_cs_t, allowed, bg_p, cg.astype(BF16), xd_p, hpg, head_dim)
        for s in range(nseq):
            rs = slice(s * t, (s + 1) * t)
            h0 = hst_ref[s, gs, :]
            y_scr[rs, gs] = y_scr[rs, gs] + _dot_nt(cg[rs].astype(BF16), h0.astype(BF16)) * dout_e[rs, gs]
            upd = _dot_tn(xdd[rs, gs].astype(BF16), bg[rs].astype(BF16))
            nhst_ref[s, gs, :] = h0 * dlast[gs, s * t:s * t + 1] + upd

    _gate_and_norm(y_scr, xs, dskip_ref, zs_ref, rmsg_ref, ycat_ref, conf_w, SSM_GROUPS, expander)


def _mix_sample_call(u, zs, xbc, dt, cst, sst, hst, seq_len, p, name):
    m, conf_w = u.shape
    ssm_w, xbc_w = zs.shape[1], xbc.shape[1]
    n_seq = m // seq_len
    n_heads = p["n_heads"]
    nseq = SAMPLE_SEQS
    rows = nseq * seq_len
    k1, k2 = p["conf_conv_w"].shape[0], p["ssm_conv_w"].shape[0]
    n_state = hst.shape[-1]
    assert seq_len == SUBLANES and n_seq % nseq == 0 and nseq % 2 == 0 and rows <= LANES
    assert conf_w // CONF_HEADS == LANES
    tile = lambda i: (i, 0)
    per_seq = lambda i: (i, 0, 0)
    const = lambda i: (0, 0)
    full = lambda a: pl.BlockSpec(a.shape, const)
    params = [p["conf_conv_w"], p["conf_conv_b"], p["conf_norm_g"], p["conf_norm_b"], p["ssm_conv_w"],
              p["ssm_conv_b"], p["a_log"], p["d_skip"], p["ssm_norm_g"]]
    ext1_rows = -(-(k1 - 1 + seq_len) // SUBLANES) * SUBLANES
    ext2_rows = -(-(k2 - 1 + seq_len) // SUBLANES) * SUBLANES
    return pl.pallas_call(
        functools.partial(_mix_sample_kernel, n_heads=n_heads, seq_len=seq_len),
        grid=(n_seq // nseq,),
        in_specs=[pl.BlockSpec((rows, conf_w), tile), pl.BlockSpec((rows, ssm_w), tile),
                  pl.BlockSpec((rows, xbc_w), tile), pl.BlockSpec((rows, LANES), tile),
                  pl.BlockSpec((nseq, k1 - 1, conf_w), per_seq),
                  pl.BlockSpec((nseq, k2 - 1, xbc_w), per_seq),
                  pl.BlockSpec((nseq, ssm_w, n_state), per_seq)] + [full(a) for a in params],
        out_specs=[pl.BlockSpec((rows, conf_w + ssm_w), tile),
                   pl.BlockSpec((nseq, k1 - 1, conf_w), per_seq),
                   pl.BlockSpec((nseq, k2 - 1, xbc_w), per_seq),
                   pl.BlockSpec((nseq, ssm_w, n_state), per_seq)],
        out_shape=[jax.ShapeDtypeStruct((m, conf_w + ssm_w), BF16),
                   jax.ShapeDtypeStruct((n_seq, k1 - 1, conf_w), F32),
                   jax.ShapeDtypeStruct((n_seq, k2 - 1, xbc_w), F32),
                   jax.ShapeDtypeStruct((n_seq, ssm_w, n_state), F32)],
        scratch_shapes=[pltpu.VMEM((nseq, ext1_rows, conf_w), F32),
                        pltpu.VMEM((nseq, ext2_rows, xbc_w), F32),
                        pltpu.VMEM((rows, xbc_w), F32),
                        pltpu.VMEM((rows, ssm_w), F32)],
        compiler_params=pltpu.CompilerParams(dimension_semantics=("arbitrary",),
                                             vmem_limit_bytes=VMEM_LIMIT_BYTES),
        name=name,
    )(u, zs, xbc, dt, cst, sst, hst, *params)


def _outproj_kernel(y_ref, x_ref, g_ref, w_ref, lg_ref, lb_ref, o_ref, *, alpha):
    tm = x_ref.shape[0]
    mixed = _dot(y_ref[...], w_ref[...])
    o_ref[...] = _layer_norm(alpha * x_ref[...] + _seq_rows(g_ref, tm) * mixed, lg_ref[...], lb_ref[...])


def _outproj_call(ycat, x, grp, w_out, ln_g, ln_b, alpha, name):
    m, d = x.shape
    kdim = ycat.shape[1]
    tm = grp.tm
    row = pl.BlockSpec((1, d), lambda i, j: (0, 0))
    return pl.pallas_call(
        functools.partial(_outproj_kernel, alpha=alpha),
        grid=(m // tm, 1),
        in_specs=[pl.BlockSpec((tm, kdim), lambda i, j: (i, 0)),
                  pl.BlockSpec((tm, d), lambda i, j: (i, 0)),
                  grp.mod_spec(d, 5),
                  pl.BlockSpec((kdim, d), lambda i, j: (0, 0), pipeline_mode=pl.Buffered(1)),
                  row, row],
        out_specs=pl.BlockSpec((tm, d), lambda i, j: (i, 0)),
        out_shape=jax.ShapeDtypeStruct((m, d), F32),
        compiler_params=pltpu.CompilerParams(dimension_semantics=("arbitrary", "arbitrary"),
                                             vmem_limit_bytes=VMEM_LIMIT_BYTES),
        name=name,
    )(ycat, x, grp.mod, w_out, ln_g.reshape(1, d), ln_b.reshape(1, d))


def kernel(x_prompt, x_sample, c_prompt, c_sample, state_conf_conv, state_ssm_conv, state_ssm, ln_in_g, ln_in_b, w_ada, b_ada, ffn1_w_gate, ffn1_w_up, ffn1_w_down, ln1_g, ln1_b, w_in, conf_conv_w, conf_conv_b, conf_norm_g, conf_norm_b, ssm_conv_w, ssm_conv_b, ssm_dt_bias, ssm_a_log, ssm_d, ssm_norm_g, w_out, ln2_g, ln2_b, ffn2_w_gate, ffn2_w_up, ffn2_w_down, ln3_g, ln3_b):
    bp, tp, d = x_prompt.shape
    bs, ts, _ = x_sample.shape
    depth = w_ada.shape[0]
    alpha = (2.0 * depth) ** 0.25
    n_heads = ssm_dt_bias.shape[-1]
    conf_w = conf_conv_w.shape[-1]
    xbc_w = ssm_conv_w.shape[-1]
    ssm_w = n_heads * SSM_HEAD_DIM
    n_state = state_ssm.shape[-1]
    head_cols = 2 * conf_w + ssm_w + xbc_w
    assert n_heads <= LANES and w_in.shape[-1] == head_cols + n_heads

    xp = x_prompt.reshape(bp * tp, d)
    xs = x_sample.reshape(bs * ts, d)
    pad_rows = -(bp + bs) % SUBLANES
    c_all = jnp.concatenate([c_prompt, c_sample, jnp.zeros((pad_rows, d), F32)], axis=0)

    def lane_pad(v):
        return jnp.zeros((1, LANES), F32).at[0, :v.shape[0]].set(v)

    outs = ([], [], [], [], [], [])
    for l in range(depth):
        mod = _ada_call(c_all, w_ada[l], b_ada[l])
        gp = _Group(mod[:bp], bp, tp, FFN_TM)
        gs = _Group(mod[bp:bp + bs], bs, ts, FFN_TM)
        w1g, w1u, w1d = ffn1_w_gate[l].astype(BF16), ffn1_w_up[l].astype(BF16), ffn1_w_down[l].astype(BF16)
        w2g, w2u, w2d = ffn2_w_gate[l].astype(BF16), ffn2_w_up[l].astype(BF16), ffn2_w_down[l].astype(BF16)
        w_in_b = w_in[l].astype(BF16)
        w_dt = jnp.zeros((d, LANES), BF16).at[:, :n_heads].set(w_in_b[:, head_cols:])
        w_out_b = w_out[l].astype(BF16)
        p = dict(n_heads=n_heads,
                 conf_conv_w=conf_conv_w[l], conf_conv_b=conf_conv_b[l].reshape(1, conf_w),
                 conf_norm_g=conf_norm_g[l].reshape(1, conf_w), conf_norm_b=conf_norm_b[l].reshape(1, conf_w),
                 ssm_conv_w=ssm_conv_w[l], ssm_conv_b=ssm_conv_b[l].reshape(1, xbc_w),
                 a_log=lane_pad(ssm_a_log[l]), d_skip=lane_pad(ssm_d[l]),
                 ssm_norm_g=ssm_norm_g[l].reshape(1, ssm_w))
        pre = (ln_in_g, ln_in_b) if l == 0 else None

        xp = _ffn_call(xp, gp, (0,), w1g, w1u, w1d, ln1_g[l], ln1_b[l], pre, alpha, "ffn1_prompt")
        xs = _ffn_call(xs, gs, (0,), w1g, w1u, w1d, ln1_g[l], ln1_b[l], pre, alpha, "ffn1_sample")

        up, zp, xbp, dtp = _inproj_call(xp, gp, w_in_b, w_dt, ssm_dt_bias[l], conf_w, ssm_w, xbc_w, "inproj_prompt")
        us, zs, xbs, dts = _inproj_call(xs, gs, w_in_b, w_dt, ssm_dt_bias[l], conf_w, ssm_w, xbc_w, "inproj_sample")

        ycp, pc, ps, ph = _mix_prompt_call(up, zp, xbp, dtp, bp, tp, p, "mix_prompt")
        ycs, sc, ss, sh = _mix_sample_call(us, zs, xbs, dts, state_conf_conv[l], state_ssm_conv[l],
                                           state_ssm[l].reshape(bs, ssm_w, n_state), ts, p, "mix_sample")

        xp = _outproj_call(ycp, xp, gp, w_out_b, ln2_g[l], ln2_b[l], alpha, "outproj_prompt")
        xs = _outproj_call(ycs, xs, gs, w_out_b, ln2_g[l], ln2_b[l], alpha, "outproj_sample")

        xp = _ffn_call(xp, gp, (6,), w2g, w2u, w2d, ln3_g[l], ln3_b[l], None, alpha, "ffn2_prompt")
        xs = _ffn_call(xs, gs, (6,), w2g, w2u, w2d, ln3_g[l], ln3_b[l], None, alpha, "ffn2_sample")

        for acc, v in zip(outs, (pc, ps, ph.reshape(bp, n_heads, SSM_HEAD_DIM, n_state),
                                 sc, ss, sh.reshape(bs, n_heads, SSM_HEAD_DIM, n_state))):
            acc.append(v)

    return (xp.reshape(bp, tp, d), xs.reshape(bs, ts, d)) + tuple(jnp.stack(a) for a in outs)
```
